```python
import math
import jax, jax.numpy as jnp
from jax import lax
import numpy as np

D_MODEL = 1024
BATCH = 16
SEQ = 4096
DEPTH = 4

DN_ALPHA = (2.0 * DEPTH) ** 0.25
DN_BETA = (8.0 * DEPTH) ** -0.25
NORM_EPS = 1e-5
N_MOD = 9
D_FF = ((8 * D_MODEL // 3 + 127) // 128) * 128

SSD_HEAD_DIM = 64
SSD_HEADS = D_MODEL // SSD_HEAD_DIM
SSD_D_INNER = SSD_HEADS * SSD_HEAD_DIM
SSD_GROUPS = 2
SSD_STATE = 128
SSD_CONV = 4
SSD_CHUNK = 128
SSD_CONV_DIM = SSD_D_INNER + 2 * SSD_GROUPS * SSD_STATE
SSD_IN = SSD_D_INNER + SSD_CONV_DIM + SSD_HEADS

POOL_WINDOWS = (2, 4, 8, 16)
POOL_GROUPS = len(POOL_WINDOWS)
POOL_GROUP_DIM = D_MODEL // 8
POOL_DIM = POOL_GROUPS * POOL_GROUP_DIM

EVEN_IN = SSD_IN + POOL_DIM
EVEN_MIX = SSD_D_INNER + POOL_DIM

CONF_DIM = D_MODEL // 2
CONF_KERNEL = 31
LRU_DIM = D_MODEL
LRU_HEADS = 8
LRU_HEAD_DIM = LRU_DIM // LRU_HEADS
LRU_CONV = 4
LRU_C = 8.0

ODD_IN = 2 * CONF_DIM + 2 * LRU_DIM
ODD_MIX = CONF_DIM + LRU_DIM

N_EVEN = (DEPTH + 1) // 2
N_ODD = DEPTH // 2

kernel_name = "hybrid_ssd_pool_conformer_rglru_trunk"


def layer_norm(x, g, b):
    xf = x.astype(jnp.float32)
    mu = jnp.mean(xf, axis=-1, keepdims=True)
    xc = xf - mu
    var = jnp.mean(xc * xc, axis=-1, keepdims=True)
    return (xc * lax.rsqrt(var + NORM_EPS) * g.astype(jnp.float32) + b.astype(jnp.float32)).astype(x.dtype)


def rms_norm(x, g):
    xf = x.astype(jnp.float32)
    ms = jnp.mean(xf * xf, axis=-1, keepdims=True)
    return xf * lax.rsqrt(ms + NORM_EPS) * g.astype(jnp.float32)


def causal_dwconv(x, w, b):
    k, ch = w.shape
    y = lax.conv_general_dilated(
        x, w[:, None, :].astype(x.dtype), window_strides=(1,), padding=[(k - 1, 0)],
        dimension_numbers=("NWC", "WIO", "NWC"), feature_group_count=ch)
    return y + b.astype(x.dtype)


def modulate(h, shift, scale):
    return h * (1.0 + scale[:, None, :]) + shift[:, None, :]


def post_norm(x, y, g, b):
    return layer_norm(DN_ALPHA * x + y, g, b)


def swiglu(h, w_in, w_out):
    gu = jnp.einsum("btd,df->btf", h, w_in)
    gate, up = jnp.split(gu, 2, axis=-1)
    return jnp.einsum("btf,fd->btd", jax.nn.silu(gate) * up, w_out)


def ssd_mixer(zxbcdt, conv_w, conv_b, dt_bias, a_log, d_skip, norm_g):
    bsz, t_len, _ = zxbcdt.shape
    z = zxbcdt[..., :SSD_D_INNER]
    xbc = zxbcdt[..., SSD_D_INNER:SSD_D_INNER + SSD_CONV_DIM]
    dt = zxbcdt[..., SSD_D_INNER + SSD_CONV_DIM:]
    xbc = jax.nn.silu(causal_dwconv(xbc, conv_w, conv_b)).astype(jnp.float32)
    g_n = SSD_GROUPS * SSD_STATE
    nc = t_len // SSD_CHUNK
    r_h = SSD_HEADS // SSD_GROUPS
    x = xbc[..., :SSD_D_INNER].reshape(bsz, nc, SSD_CHUNK, SSD_GROUPS, r_h, SSD_HEAD_DIM)
    bm = xbc[..., SSD_D_INNER:SSD_D_INNER + g_n].reshape(bsz, nc, SSD_CHUNK, SSD_GROUPS, SSD_STATE)
    cm = xbc[..., SSD_D_INNER + g_n:].reshape(bsz, nc, SSD_CHUNK, SSD_GROUPS, SSD_STATE)
    dt = jax.nn.softplus(dt.astype(jnp.float32) + dt_bias.astype(jnp.float32))
    dt = dt.reshape(bsz, nc, SSD_CHUNK, SSD_GROUPS, r_h)
    a = -jnp.exp(a_log.astype(jnp.float32)).reshape(SSD_GROUPS, r_h)
    a_cum = jnp.cumsum(dt * a, axis=2)
    xdt = x * dt[..., None]
    seg = a_cum[:, :, :, None] - a_cum[:, :, None]
    causal = jnp.tril(jnp.ones((SSD_CHUNK, SSD_CHUNK), dtype=bool))[:, :, None, None]
    l_mat = jnp.exp(jnp.where(causal, seg, -jnp.inf))
    cb = jnp.einsum("bclgn,bcsgn->bclsg", cm, bm)
    y_diag = jnp.einsum("bclsg,bclsgr,bcsgrp->bclgrp", cb, l_mat, xdt)
    decay_s = jnp.exp(a_cum[:, :, -1:] - a_cum)
    states = jnp.einsum("bcsgn,bcsgr,bcsgrp->bcgrpn", bm, decay_s, xdt)
    chunk_decay = jnp.exp(a_cum[:, :, -1])

    def step(h, inp):
        s_c, dec = inp
        return h * dec[..., None, None] + s_c, h

    h0 = jnp.zeros((bsz, SSD_GROUPS, r_h, SSD_HEAD_DIM, SSD_STATE), jnp.float32)
    _, h_prev = lax.scan(step, h0, (jnp.moveaxis(states, 1, 0), jnp.moveaxis(chunk_decay, 1, 0)))
    h_prev = jnp.moveaxis(h_prev, 0, 1)
    y_off = jnp.einsum("bclgn,bcgrpn,bclgr->bclgrp", cm, h_prev, jnp.exp(a_cum))
    y = y_diag + y_off + x * d_skip.astype(jnp.float32).reshape(SSD_GROUPS, r_h)[:, :, None]
    y = y.reshape(bsz, t_len, SSD_D_INNER) * jax.nn.silu(z.astype(jnp.float32))
    return rms_norm(y, norm_g).astype(zxbcdt.dtype)


def pool_mixer(u, w_grp, scale):
    bsz, t_len, _ = u.shape
    uf = u.astype(jnp.float32).reshape(bsz, t_len, POOL_GROUPS, POOL_GROUP_DIM)
    cs = jnp.cumsum(uf, axis=1)
    pos = jnp.arange(1, t_len + 1, dtype=jnp.float32)
    outs = []
    for g, w in enumerate(POOL_WINDOWS):
        c_g = cs[:, :, g]
        lo = jnp.pad(c_g[:, :t_len - w], ((0, 0), (w, 0), (0, 0)))
        cnt = jnp.minimum(pos, float(w))[None, :, None]
        outs.append((c_g - lo) / cnt - uf[:, :, g])
    pooled = jnp.stack(outs, axis=2)
    mixed = jnp.einsum("btgc,gcd->btgd", pooled, w_grp.astype(jnp.float32))
    return (mixed.reshape(bsz, t_len, POOL_DIM) * scale.astype(jnp.float32)).astype(u.dtype)


def conformer_conv(v, gate, dw_w, dw_b, ln_g, ln_b):
    h = v * jax.nn.sigmoid(gate)
    h = causal_dwconv(h, dw_w, dw_b)
    h = layer_norm(h, ln_g, ln_b)
    return jax.nn.silu(h)


def _lin_combine(e1, e2):
    a1, b1 = e1
    a2, b2 = e2
    return a1 * a2, a2 * b1 + b2


def rglru_mixer(xr, gate, conv_w, conv_b, wa, ba, wx, bx, lam):
    bsz, t_len, _ = xr.shape
    xr = causal_dwconv(xr, conv_w, conv_b)
    xh = xr.reshape(bsz, t_len, LRU_HEADS, LRU_HEAD_DIM)
    r = jax.nn.sigmoid(jnp.einsum("bthi,hij->bthj", xh, wa).reshape(bsz, t_len, LRU_DIM) + ba)
    i = jax.nn.sigmoid(jnp.einsum("bthi,hij->bthj", xh, wx).reshape(bsz, t_len, LRU_DIM) + bx)
    log_a = -LRU_C * r.astype(jnp.float32) * jax.nn.softplus(-lam.astype(jnp.float32))
    a = jnp.exp(log_a)
    b = jnp.sqrt(-jnp.expm1(2.0 * log_a)) * (i * xr).astype(jnp.float32)
    _, h = lax.associative_scan(_lin_combine, (a, b), axis=1)
    return h.astype(xr.dtype) * jax.nn.gelu(gate)


def setup_inputs(seed: int = 0) -> dict:
    key = jax.random.key(seed)
    ks = iter(jax.random.split(key, 48))

    def nrm(shape, scale):
        return jax.random.normal(next(ks), shape, jnp.float32) * scale

    def unif(shape, lo, hi):
        return jax.random.uniform(next(ks), shape, jnp.float32, minval=lo, maxval=hi)

    x = nrm((BATCH, SEQ, D_MODEL), 1.0)
    c = nrm((BATCH, D_MODEL), 1.0)
    ada_w = nrm((DEPTH, D_MODEL, N_MOD * D_MODEL), 0.1 * D_MODEL ** -0.5)
    ada_b = nrm((DEPTH, N_MOD * D_MODEL), 0.01)
    ln_g = 1.0 + nrm((DEPTH, 3, D_MODEL), 0.01)
    ln_b = nrm((DEPTH, 3, D_MODEL), 0.01)
    ffn_w_in = nrm((DEPTH, 2, D_MODEL, 2 * D_FF), D_MODEL ** -0.5)
    ffn_w_out = nrm((DEPTH, 2, D_FF, D_MODEL), DN_BETA * D_FF ** -0.5)
    ev_w_in = nrm((N_EVEN, D_MODEL, EVEN_IN), D_MODEL ** -0.5)
    ssd_conv_w = nrm((N_EVEN, SSD_CONV, SSD_CONV_DIM), SSD_CONV ** -0.5)
    ssd_conv_b = nrm((N_EVEN, SSD_CONV_DIM), 0.01)
    dt0 = jnp.exp(unif((N_EVEN, SSD_HEADS), math.log(1e-3), math.log(1e-1)))
    ssd_dt_bias = dt0 + jnp.log(-jnp.expm1(-dt0))
    ssd_a_log = jnp.log(unif((N_EVEN, SSD_HEADS), 1.0, 16.0))
    ssd_d = 1.0 + nrm((N_EVEN, SSD_HEADS), 0.01)
    ssd_norm_g = 1.0 + nrm((N_EVEN, SSD_D_INNER), 0.01)
    pool_w = nrm((N_EVEN, POOL_GROUPS, POOL_GROUP_DIM, POOL_GROUP_DIM), POOL_GROUP_DIM ** -0.5)
    pool_scale = 1.0 + nrm((N_EVEN, POOL_DIM), 0.01)
    ev_w_out = nrm((N_EVEN, EVEN_MIX, D_MODEL), DN_BETA * EVEN_MIX ** -0.5)
    od_w_in = nrm((N_ODD, D_MODEL, ODD_IN), D_MODEL ** -0.5)
    conf_dw_w = nrm((N_ODD, CONF_KERNEL, CONF_DIM), CONF_KERNEL ** -0.5)
    conf_dw_b = nrm((N_ODD, CONF_DIM), 0.01)
    conf_ln_g = 1.0 + nrm((N_ODD, CONF_DIM), 0.01)
    conf_ln_b = nrm((N_ODD, CONF_DIM), 0.01)
    lru_conv_w = nrm((N_ODD, LRU_CONV, LRU_DIM), LRU_CONV ** -0.5)
    lru_conv_b = nrm((N_ODD, LRU_DIM), 0.01)
    lru_wa = nrm((N_ODD, LRU_HEADS, LRU_HEAD_DIM, LRU_HEAD_DIM), LRU_HEAD_DIM ** -0.5)
    lru_ba = nrm((N_ODD, LRU_DIM), 0.01)
    lru_wx = nrm((N_ODD, LRU_HEADS, LRU_HEAD_DIM, LRU_HEAD_DIM), LRU_HEAD_DIM ** -0.5)
    lru_bx = nrm((N_ODD, LRU_DIM), 0.01)
    a_c = unif((N_ODD, LRU_DIM), 0.9, 0.999)
    a_base = a_c ** (1.0 / LRU_C)
    lru_lambda = jnp.log(a_base) - jnp.log1p(-a_base)
    od_w_out = nrm((N_ODD, ODD_MIX, D_MODEL), DN_BETA * ODD_MIX ** -0.5)
    return {
        "x": x, "c": c, "ada_w": ada_w, "ada_b": ada_b, "ln_g": ln_g, "ln_b": ln_b,
        "ffn_w_in": ffn_w_in, "ffn_w_out": ffn_w_out,
        "ev_w_in": ev_w_in, "ssd_conv_w": ssd_conv_w, "ssd_conv_b": ssd_conv_b,
        "ssd_dt_bias": ssd_dt_bias, "ssd_a_log": ssd_a_log, "ssd_d": ssd_d, "ssd_norm_g": ssd_norm_g,
        "pool_w": pool_w, "pool_scale": pool_scale, "ev_w_out": ev_w_out,
        "od_w_in": od_w_in, "conf_dw_w": conf_dw_w, "conf_dw_b": conf_dw_b,
        "conf_ln_g": conf_ln_g, "conf_ln_b": conf_ln_b,
        "lru_conv_w": lru_conv_w, "lru_conv_b": lru_conv_b, "lru_wa": lru_wa, "lru_ba": lru_ba,
        "lru_wx": lru_wx, "lru_bx": lru_bx, "lru_lambda": lru_lambda, "od_w_out": od_w_out,
    }


def reference(x, c, ada_w, ada_b, ln_g, ln_b, ffn_w_in, ffn_w_out,
              ev_w_in, ssd_conv_w, ssd_conv_b, ssd_dt_bias, ssd_a_log, ssd_d, ssd_norm_g,
              pool_w, pool_scale, ev_w_out,
              od_w_in, conf_dw_w, conf_dw_b, conf_ln_g, conf_ln_b,
              lru_conv_w, lru_conv_b, lru_wa, lru_ba, lru_wx, lru_bx, lru_lambda, od_w_out):
    cond = jax.nn.silu(c)
    for layer in range(DEPTH):
        mod = cond @ ada_w[layer] + ada_b[layer]
        sh1, sc1, g1, sh2, sc2, g2, sh3, sc3, g3 = jnp.split(mod, N_MOD, axis=-1)
        y = swiglu(modulate(x, sh1, sc1), ffn_w_in[layer, 0], ffn_w_out[layer, 0])
        x = post_norm(x, 0.5 * (1.0 + g1[:, None, :]) * y, ln_g[layer, 0], ln_b[layer, 0])
        h = modulate(x, sh2, sc2)
        if layer % 2 == 0:
            e = layer // 2
            proj = jnp.einsum("btd,de->bte", h, ev_w_in[e])
            y_a = ssd_mixer(proj[..., :SSD_IN], ssd_conv_w[e], ssd_conv_b[e], ssd_dt_bias[e],
                            ssd_a_log[e], ssd_d[e], ssd_norm_g[e])
            y_b = pool_mixer(proj[..., SSD_IN:], pool_w[e], pool_scale[e])
            y = jnp.einsum("bte,ed->btd", jnp.concatenate([y_a, y_b], axis=-1), ev_w_out[e])
        else:
            o = layer // 2
            proj = jnp.einsum("btd,de->bte", h, od_w_in[o])
            v = proj[..., :CONF_DIM]
            gt = proj[..., CONF_DIM:2 * CONF_DIM]
            xr = proj[..., 2 * CONF_DIM:2 * CONF_DIM + LRU_DIM]
            gr = proj[..., 2 * CONF_DIM + LRU_DIM:]
            y_c = conformer_conv(v, gt, conf_dw_w[o], conf_dw_b[o], conf_ln_g[o], conf_ln_b[o])
            y_d = rglru_mixer(xr, gr, lru_conv_w[o], lru_conv_b[o], lru_wa[o], lru_ba[o],
                              lru_wx[o], lru_bx[o], lru_lambda[o])
            y = jnp.einsum("bte,ed->btd", jnp.concatenate([y_c, y_d], axis=-1), od_w_out[o])
        x = post_norm(x, (1.0 + g2[:, None, :]) * y, ln_g[layer, 1], ln_b[layer, 1])
        y = swiglu(modulate(x, sh3, sc3), ffn_w_in[layer, 1], ffn_w_out[layer, 1])
        x = post_norm(x, 0.5 * (1.0 + g3[:, None, :]) * y, ln_g[layer, 2], ln_b[layer, 2])
    return x
```

```python
import functools

import jax
import jax.numpy as jnp
from jax import lax
from jax.experimental import pallas as pl
from jax.experimental.pallas import tpu as pltpu

F32 = jnp.float32
BF16 = jnp.bfloat16

D_MODEL = 1024
DEPTH = 4
DN_ALPHA = (2.0 * DEPTH) ** 0.25
NORM_EPS = 1e-5
N_MOD = 9
D_FF = ((8 * D_MODEL // 3 + 127) // 128) * 128

SSD_HEAD_DIM = 64
SSD_HEADS = D_MODEL // SSD_HEAD_DIM
SSD_D_INNER = SSD_HEADS * SSD_HEAD_DIM
SSD_GROUPS = 2
SSD_STATE = 128
SSD_CONV = 4
SSD_CHUNK = 128
SSD_CONV_DIM = SSD_D_INNER + 2 * SSD_GROUPS * SSD_STATE
SSD_GROUP_DIM = SSD_D_INNER // SSD_GROUPS

POOL_WINDOWS = (2, 4, 8, 16)
POOL_GROUP_DIM = D_MODEL // 8
POOL_DIM = len(POOL_WINDOWS) * POOL_GROUP_DIM
POOL_HALO = 16

EVEN_MIX = SSD_D_INNER + POOL_DIM

CONF_DIM = D_MODEL // 2
CONF_KERNEL = 31
CONF_HALO = 32
LRU_DIM = D_MODEL
LRU_HEADS = 8
LRU_HEAD_DIM = LRU_DIM // LRU_HEADS
LRU_CONV = 4
LRU_C = 8.0
ODD_IN = 2 * CONF_DIM + 2 * LRU_DIM
ODD_MIX = CONF_DIM + LRU_DIM

LANES = 128
SUBLANES = 8
CONV_HALO = SUBLANES
DT_PAD = LANES
EVEN_IN_PAD = SSD_D_INNER + SSD_CONV_DIM + POOL_DIM + DT_PAD

TOKEN_TILE = 512
VMEM_LIMIT = 56 * 1024 * 1024


def _const_spec(block_shape, index):
    return pl.BlockSpec(block_shape, lambda b, t: index, pipeline_mode=pl.Buffered(1))


def _params():
    return pltpu.CompilerParams(dimension_semantics=("arbitrary", "arbitrary"),
                                vmem_limit_bytes=VMEM_LIMIT)


def _dot(a, b):
    return jnp.dot(a, b, preferred_element_type=F32)


def _softplus(x):
    return jnp.maximum(x, 0.0) + jnp.log1p(jnp.exp(-jnp.abs(x)))


def _expm1(x):
    u = jnp.exp(x)
    um1 = u - 1.0
    safe = jnp.logical_and(um1 != 0.0, um1 != -1.0)
    y = um1 * x / jnp.log(jnp.where(safe, u, 2.0))
    return jnp.where(um1 == 0.0, x, jnp.where(um1 == -1.0, -1.0, y))


def _split3(x):
    hi = x.astype(BF16)
    r1 = x - hi.astype(F32)
    mid = r1.astype(BF16)
    lo = (r1 - mid.astype(F32)).astype(BF16)
    return hi, mid, lo


def _dot_exact_lhs(sel, x):
    hi, mid, lo = _split3(x)
    return (_dot(sel, lo) + _dot(sel, mid)) + _dot(sel, hi)


def _dot_exact_rhs(x, sel):
    hi, mid, lo = _split3(x)
    return (_dot(lo, sel) + _dot(mid, sel)) + _dot(hi, sel)


def _modulate(x, mod_ref, k):
    shift = mod_ref[0, 0, 3 * k:3 * k + 1, :]
    scale = mod_ref[0, 0, 3 * k + 1:3 * k + 2, :]
    return x * (1.0 + scale) + shift


def _post_norm(x, y, gate, g, b):
    v = DN_ALPHA * x + gate * y
    mu = jnp.mean(v, axis=-1, keepdims=True)
    vc = v - mu
    var = jnp.mean(vc * vc, axis=-1, keepdims=True)
    return vc * lax.rsqrt(var + NORM_EPS) * g + b


def _mod_kernel(c_ref, w_ref, b_ref, o_ref):
    cond = jax.nn.silu(c_ref[...]).astype(BF16)
    o_ref[0] = _dot(cond, w_ref[0].astype(BF16)) + b_ref[0]


def _ada_mod(c, ada_w, ada_b):
    bsz = c.shape[0]
    n_out = N_MOD * D_MODEL
    nt = n_out // 8
    return pl.pallas_call(
        _mod_kernel,
        grid=(DEPTH, n_out // nt),
        in_specs=[
            pl.BlockSpec((bsz, D_MODEL), lambda l, j: (0, 0)),
            pl.BlockSpec((1, D_MODEL, nt), lambda l, j: (l, 0, j)),
            pl.BlockSpec((1, 1, nt), lambda l, j: (l, 0, j)),
        ],
        out_specs=pl.BlockSpec((1, bsz, nt), lambda l, j: (l, 0, j)),
        out_shape=jax.ShapeDtypeStruct((DEPTH, bsz, n_out), F32),
        compiler_params=_params(),
        name="ada_mod",
    )(c, ada_w, ada_b.reshape(DEPTH, 1, n_out))


def _ffn_kernel(sub, x_ref, mod_ref, win_ref, wout_ref, lg_ref, lb_ref, o_ref):
    x = x_ref[0]
    h = _modulate(x, mod_ref, sub).astype(BF16)
    gu = _dot(h, win_ref[0, 0])
    act = (jax.nn.silu(gu[:, :D_FF]) * gu[:, D_FF:]).astype(BF16)
    y = _dot(act, wout_ref[0, 0])
    gate = 0.5 * (1.0 + mod_ref[0, 0, 3 * sub + 2:3 * sub + 3, :])
    o_ref[0] = _post_norm(x, y, gate, lg_ref[0, sub:sub + 1, :], lb_ref[0, sub:sub + 1, :])


def _ffn(x, mod, w_in, w_out, ln_g, ln_b, layer, which):
    bsz, t_len, _ = x.shape
    tm = min(TOKEN_TILE, t_len)
    sub = 0 if which == 0 else 2
    return pl.pallas_call(
        functools.partial(_ffn_kernel, sub),
        grid=(bsz, t_len // tm),
        in_specs=[
            pl.BlockSpec((1, tm, D_MODEL), lambda b, t: (b, t, 0)),
            pl.BlockSpec((1, 1, N_MOD, D_MODEL), lambda b, t: (layer, b, 0, 0)),
            _const_spec((1, 1, D_MODEL, 2 * D_FF), (layer, which, 0, 0)),
            _const_spec((1, 1, D_FF, D_MODEL), (layer, which, 0, 0)),
            _const_spec((1, 3, D_MODEL), (layer, 0, 0)),
            _const_spec((1, 3, D_MODEL), (layer, 0, 0)),
        ],
        out_specs=pl.BlockSpec((1, tm, D_MODEL), lambda b, t: (b, t, 0)),
        out_shape=jax.ShapeDtypeStruct(x.shape, F32),
        compiler_params=_params(),
        name="ffn",
    )(x, mod, w_in, w_out, ln_g, ln_b)


def _ssd_chunk(c, tri, expand, causal, a_row, dsk, ng, proj_buf, conv_buf, dt_buf, mix_buf, state_ref):
    q = SSD_CHUNK
    r0 = pl.multiple_of(c * q, q)
    rows = pl.ds(r0, q)
    dtc = dt_buf[rows, :]
    acs = _dot_exact_lhs(tri, dtc * a_row)
    acs_t = acs.T
    last = acs[q - 1:q, :]
    ea = jnp.exp(acs)
    dec = jnp.exp(last - acs)
    dt_e = _dot(dtc.astype(BF16), expand)
    dec_e = _dot(dec.astype(BF16), expand)
    ea_e = _dot(ea.astype(BF16), expand)
    cd_e = _dot_exact_rhs(jnp.broadcast_to(jnp.exp(last), (SUBLANES, DT_PAD)), expand)[0:1, :]

    xs = conv_buf[rows, 0:SSD_D_INNER]
    xdt = xs * dt_e
    xdt_b = xdt.astype(BF16)
    xd_b = (xdt * dec_e).astype(BF16)
    lane = lax.broadcasted_iota(jnp.int32, (q, LANES), 1)
    first_head = lane < SSD_HEAD_DIM

    y_groups = []
    for g in range(SSD_GROUPS):
        b0 = SSD_D_INNER + g * SSD_STATE
        c0 = SSD_D_INNER + SSD_GROUPS * SSD_STATE + g * SSD_STATE
        bg_t = conv_buf[rows, b0:b0 + SSD_STATE].T.astype(BF16)
        cg = conv_buf[rows, c0:c0 + SSD_STATE].astype(BF16)
        cb = _dot(cg, bg_t)
        gsl = slice(g * SSD_GROUP_DIM, (g + 1) * SSD_GROUP_DIM)
        blocks = []
        for j in range(SSD_GROUP_DIM // LANES):
            m_pair = []
            for hh in range(2):
                head = g * (SSD_HEADS // SSD_GROUPS) + 2 * j + hh
                seg = acs[:, head:head + 1] - acs_t[head:head + 1, :]
                decay = jnp.exp(jnp.where(causal, seg, -jnp.inf))
                m_pair.append((cb * decay).astype(BF16))
            blk = xdt_b[:, g * SSD_GROUP_DIM + j * LANES:g * SSD_GROUP_DIM + (j + 1) * LANES]
            zero = jnp.zeros_like(blk)
            rhs = jnp.concatenate([jnp.where(first_head, blk, zero), jnp.where(first_head, zero, blk)], axis=0)
            blocks.append(_dot(jnp.concatenate(m_pair, axis=1), rhs))
        y_diag = jnp.concatenate(blocks, axis=1)
        h_prev = state_ref[g]
        y_off = _dot(cg, h_prev.astype(BF16)) * ea_e[:, gsl]
        state_ref[g] = h_prev * cd_e[:, gsl] + _dot(bg_t, xd_b[:, gsl])
        y_groups.append(y_diag + y_off)
    y = jnp.concatenate(y_groups, axis=1) + xs * dsk
    y = y * jax.nn.silu(proj_buf[rows, 0:SSD_D_INNER])
    ms = jnp.mean(y * y, axis=-1, keepdims=True)
    mix_buf[rows, 0:SSD_D_INNER] = (y * lax.rsqrt(ms + NORM_EPS) * ng).astype(BF16)


def _even_kernel(x_ref, mod_ref, win_ref, cw_ref, cb_ref, dtb_ref, alog_ref, dsk_ref, ng_ref,
                 pw_ref, ps_ref, wout_ref, lg_ref, lb_ref, o_ref,
                 proj_buf, xbc_buf, conv_buf, dt_buf, u_buf, mix_buf, state_ref):
    t = pl.program_id(1)
    tm = x_ref.shape[1]

    @pl.when(t == 0)
    def _():
        xbc_buf[0:CONV_HALO, :] = jnp.zeros((CONV_HALO, SSD_CONV_DIM), F32)
        u_buf[0:POOL_HALO, :] = jnp.zeros((POOL_HALO, POOL_DIM), F32)
        state_ref[...] = jnp.zeros(state_ref.shape, F32)

    x = x_ref[0]
    h = _modulate(x, mod_ref, 1).astype(BF16)
    proj_buf[...] = _dot(h, win_ref[0])
    o_xbc = SSD_D_INNER
    o_u = o_xbc + SSD_CONV_DIM
    o_dt = o_u + POOL_DIM

    xbc_buf[CONV_HALO:CONV_HALO + tm, :] = proj_buf[:, o_xbc:o_u]
    acc = jnp.broadcast_to(cb_ref[0], (tm, SSD_CONV_DIM))
    for k in range(SSD_CONV):
        acc = acc + cw_ref[0, k:k + 1, :] * xbc_buf[pl.ds(CONV_HALO - (SSD_CONV - 1) + k, tm), :]
    conv_buf[...] = jax.nn.silu(acc)
    xbc_buf[0:CONV_HALO, :] = xbc_buf[tm:tm + CONV_HALO, :]

    dt_buf[...] = _softplus(proj_buf[:, o_dt:o_dt + DT_PAD] + dtb_ref[0])

    q = SSD_CHUNK
    r_i = lax.broadcasted_iota(jnp.int32, (q, q), 0)
    c_i = lax.broadcasted_iota(jnp.int32, (q, q), 1)
    causal = r_i >= c_i
    tri = causal.astype(BF16)
    e_r = lax.broadcasted_iota(jnp.int32, (DT_PAD, SSD_D_INNER), 0)
    e_c = lax.broadcasted_iota(jnp.int32, (DT_PAD, SSD_D_INNER), 1)
    expand = (e_r == e_c // SSD_HEAD_DIM).astype(BF16)
    head_lane = lax.broadcasted_iota(jnp.int32, (1, DT_PAD), 1) < SSD_HEADS
    a_row = jnp.where(head_lane, -jnp.exp(alog_ref[0]), 0.0)
    dsk = dsk_ref[0]
    ng = ng_ref[0]

    def chunk_body(c, carry):
        _ssd_chunk(c, tri, expand, causal, a_row, dsk, ng, proj_buf, conv_buf, dt_buf, mix_buf, state_ref)
        return carry

    lax.fori_loop(0, tm // q, chunk_body, 0)

    u_buf[POOL_HALO:POOL_HALO + tm, :] = proj_buf[:, o_u:o_dt]
    pos = (t * tm + lax.broadcasted_iota(jnp.int32, (tm, 1), 0) + 1).astype(F32)
    for g, w in enumerate(POOL_WINDOWS):
        cols = slice(g * POOL_GROUP_DIM, (g + 1) * POOL_GROUP_DIM)
        cur = u_buf[pl.ds(POOL_HALO, tm), cols]
        s = cur
        for j in range(1, w):
            s = s + u_buf[pl.ds(POOL_HALO - j, tm), cols]
        pooled = s / jnp.minimum(pos, float(w)) - cur
        mixed = _dot(pooled.astype(BF16), pw_ref[0, g])
        mix_buf[:, SSD_D_INNER + g * POOL_GROUP_DIM:SSD_D_INNER + (g + 1) * POOL_GROUP_DIM] = (
            mixed * ps_ref[0, :, cols]).astype(BF16)
    u_buf[0:POOL_HALO, :] = u_buf[tm:tm + POOL_HALO, :]

    y = _dot(mix_buf[...], wout_ref[0])
    gate = 1.0 + mod_ref[0, 0, 5:6, :]
    o_ref[0] = _post_norm(x, y, gate, lg_ref[0, 1:2, :], lb_ref[0, 1:2, :])


def _even_mixer(x, mod, w_in, conv_w, conv_b, dt_bias, a_log, d_skip, norm_g, pool_w, pool_scale,
                w_out, ln_g, ln_b, layer):
    bsz, t_len, _ = x.shape
    tm = min(TOKEN_TILE, t_len)
    e = layer // 2
    return pl.pallas_call(
        _even_kernel,
        grid=(bsz, t_len // tm),
        in_specs=[
            pl.BlockSpec((1, tm, D_MODEL), lambda b, t: (b, t, 0)),
            pl.BlockSpec((1, 1, N_MOD, D_MODEL), lambda b, t: (layer, b, 0, 0)),
            _const_spec((1, D_MODEL, EVEN_IN_PAD), (e, 0, 0)),
            _const_spec((1, SSD_CONV, SSD_CONV_DIM), (e, 0, 0)),
            _const_spec((1, 1, SSD_CONV_DIM), (e, 0, 0)),
            _const_spec((1, 1, DT_PAD), (e, 0, 0)),
            _const_spec((1, 1, DT_PAD), (e, 0, 0)),
            _const_spec((1, 1, SSD_D_INNER), (e, 0, 0)),
            _const_spec((1, 1, SSD_D_INNER), (e, 0, 0)),
            _const_spec((1, len(POOL_WINDOWS), POOL_GROUP_DIM, POOL_GROUP_DIM), (e, 0, 0, 0)),
            _const_spec((1, 1, POOL_DIM), (e, 0, 0)),
            _const_spec((1, EVEN_MIX, D_MODEL), (e, 0, 0)),
            _const_spec((1, 3, D_MODEL), (layer, 0, 0)),
            _const_spec((1, 3, D_MODEL), (layer, 0, 0)),
        ],
        out_specs=pl.BlockSpec((1, tm, D_MODEL), lambda b, t: (b, t, 0)),
        out_shape=jax.ShapeDtypeStruct(x.shape, F32),
        scratch_shapes=[
            pltpu.VMEM((tm, EVEN_IN_PAD), F32),
            pltpu.VMEM((CONV_HALO + tm, SSD_CONV_DIM), F32),
            pltpu.VMEM((tm, SSD_CONV_DIM), F32),
            pltpu.VMEM((tm, DT_PAD), F32),
            pltpu.VMEM((POOL_HALO + tm, POOL_DIM), F32),
            pltpu.VMEM((tm, EVEN_MIX), BF16),
            pltpu.VMEM((SSD_GROUPS, SSD_STATE, SSD_GROUP_DIM), F32),
        ],
        compiler_params=_params(),
        name="even_mixer",
    )(x, mod, w_in, conv_w, conv_b, dt_bias, a_log, d_skip, norm_g, pool_w, pool_scale, w_out, ln_g, ln_b)


def _odd_kernel(x_ref, mod_ref, win_ref, dww_ref, dwb_ref, cg_ref, cbeta_ref, lcw_ref, lcb_ref,
                wax_ref, ba_ref, bx_ref, lam_ref, wout_ref, lg_ref, lb_ref, o_ref,
                proj_buf, glu_buf, xr_buf, a_buf, b_buf, mix_buf, carry_ref):
    t = pl.program_id(1)
    tm = x_ref.shape[1]

    @pl.when(t == 0)
    def _():
        glu_buf[0:CONF_HALO, :] = jnp.zeros((CONF_HALO, CONF_DIM), F32)
        xr_buf[0:CONV_HALO, :] = jnp.zeros((CONV_HALO, LRU_DIM), F32)
        carry_ref[...] = jnp.zeros(carry_ref.shape, F32)

    x = x_ref[0]
    h = _modulate(x, mod_ref, 1).astype(BF16)
    proj_buf[...] = _dot(h, win_ref[0])
    o_gt = CONF_DIM
    o_xr = 2 * CONF_DIM
    o_gr = o_xr + LRU_DIM

    glu_buf[CONF_HALO:CONF_HALO + tm, :] = proj_buf[:, 0:o_gt] * jax.nn.sigmoid(proj_buf[:, o_gt:o_xr])
    acc = jnp.broadcast_to(dwb_ref[0], (tm, CONF_DIM))
    for k in range(CONF_KERNEL):
        acc = acc + dww_ref[0, k:k + 1, :] * glu_buf[pl.ds(CONF_HALO - (CONF_KERNEL - 1) + k, tm), :]
    glu_buf[0:CONF_HALO, :] = glu_buf[tm:tm + CONF_HALO, :]
    mu = jnp.mean(acc, axis=-1, keepdims=True)
    ac = acc - mu
    var = jnp.mean(ac * ac, axis=-1, keepdims=True)
    mix_buf[:, 0:CONF_DIM] = jax.nn.silu(ac * lax.rsqrt(var + NORM_EPS) * cg_ref[0] + cbeta_ref[0]).astype(BF16)

    xr_buf[CONV_HALO:CONV_HALO + tm, :] = proj_buf[:, o_xr:o_gr]
    xc = jnp.broadcast_to(lcb_ref[0], (tm, LRU_DIM))
    for k in range(LRU_CONV):
        xc = xc + lcw_ref[0, k:k + 1, :] * xr_buf[pl.ds(CONV_HALO - (LRU_CONV - 1) + k, tm), :]
    xr_buf[0:CONV_HALO, :] = xr_buf[tm:tm + CONV_HALO, :]
    xc_b = xc.astype(BF16)
    r_parts, i_parts = [], []
    for hd in range(LRU_HEADS):
        cols = slice(hd * LRU_HEAD_DIM, (hd + 1) * LRU_HEAD_DIM)
        ri = _dot(xc_b[:, cols], wax_ref[0, hd])
        r_parts.append(ri[:, :LRU_HEAD_DIM])
        i_parts.append(ri[:, LRU_HEAD_DIM:])
    r = jax.nn.sigmoid(jnp.concatenate(r_parts, axis=1) + ba_ref[0])
    i = jax.nn.sigmoid(jnp.concatenate(i_parts, axis=1) + bx_ref[0])
    log_a = -LRU_C * r * _softplus(-lam_ref[0])
    a = jnp.exp(log_a)
    b = jnp.sqrt(-_expm1(2.0 * log_a)) * (i * xc)

    row = lax.broadcasted_iota(jnp.int32, (tm, 1), 0) % SUBLANES
    for s in (1, 2, 4):
        keep = row >= s
        a_prev = pltpu.roll(a, s, 0)
        b_prev = pltpu.roll(b, s, 0)
        b = jnp.where(keep, a * b_prev + b, b)
        a = jnp.where(keep, a * a_prev, a)
    a_buf[...] = a
    b_buf[...] = b

    def group_body(gi, carry):
        rows = pl.ds(pl.multiple_of(gi * SUBLANES, SUBLANES), SUBLANES)
        hg = a_buf[rows, :] * carry + b_buf[rows, :]
        b_buf[rows, :] = hg
        return jnp.broadcast_to(hg[SUBLANES - 1:SUBLANES, :], (SUBLANES, LRU_DIM))

    carry_ref[...] = lax.fori_loop(0, tm // SUBLANES, group_body, carry_ref[...])
    mix_buf[:, CONF_DIM:ODD_MIX] = (b_buf[...] * jax.nn.gelu(proj_buf[:, o_gr:ODD_IN])).astype(BF16)

    y = _dot(mix_buf[...], wout_ref[0])
    gate = 1.0 + mod_ref[0, 0, 5:6, :]
    o_ref[0] = _post_norm(x, y, gate, lg_ref[0, 1:2, :], lb_ref[0, 1:2, :])


def _odd_mixer(x, mod, w_in, dw_w, dw_b, conf_g, conf_b, lru_cw, lru_cb, wax, ba, bx, lam,
               w_out, ln_g, ln_b, layer):
    bsz, t_len, _ = x.shape
    tm = min(TOKEN_TILE, t_len)
    o = layer // 2
    return pl.pallas_call(
        _odd_kernel,
        grid=(bsz, t_len // tm),
        in_specs=[
            pl.BlockSpec((1, tm, D_MODEL), lambda b, t: (b, t, 0)),
            pl.BlockSpec((1, 1, N_MOD, D_MODEL), lambda b, t: (layer, b, 0, 0)),
            _const_spec((1, D_MODEL, ODD_IN), (o, 0, 0)),
            _const_spec((1, CONF_KERNEL, CONF_DIM), (o, 0, 0)),
            _const_spec((1, 1, CONF_DIM), (o, 0, 0)),
            _const_spec((1, 1, CONF_DIM), (o, 0, 0)),
            _const_spec((1, 1, CONF_DIM), (o, 0, 0)),
            _const_spec((1, LRU_CONV, LRU_DIM), (o, 0, 0)),
            _const_spec((1, 1, LRU_DIM), (o, 0, 0)),
            _const_spec((1, LRU_HEADS, LRU_HEAD_DIM, 2 * LRU_HEAD_DIM), (o, 0, 0, 0)),
            _const_spec((1, 1, LRU_DIM), (o, 0, 0)),
            _const_spec((1, 1, LRU_DIM), (o, 0, 0)),
            _const_spec((1, 1, LRU_DIM), (o, 0, 0)),
            _const_spec((1, ODD_MIX, D_MODEL), (o, 0, 0)),
            _const_spec((1, 3, D_MODEL), (layer, 0, 0)),
            _const_spec((1, 3, D_MODEL), (layer, 0, 0)),
        ],
        out_specs=pl.BlockSpec((1, tm, D_MODEL), lambda b, t: (b, t, 0)),
        out_shape=jax.ShapeDtypeStruct(x.shape, F32),
        scratch_shapes=[
            pltpu.VMEM((tm, ODD_IN), F32),
            pltpu.VMEM((CONF_HALO + tm, CONF_DIM), F32),
            pltpu.VMEM((CONV_HALO + tm, LRU_DIM), F32),
            pltpu.VMEM((tm, LRU_DIM), F32),
            pltpu.VMEM((tm, LRU_DIM), F32),
            pltpu.VMEM((tm, ODD_MIX), BF16),
            pltpu.VMEM((SUBLANES, LRU_DIM), F32),
        ],
        compiler_params=_params(),
        name="odd_mixer",
    )(x, mod, w_in, dw_w, dw_b, conf_g, conf_b, lru_cw, lru_cb, wax, ba, bx, lam, w_out, ln_g, ln_b)


def _row(v):
    return v.reshape(v.shape[0], 1, v.shape[1])


def _pad_lanes(v, width):
    return jnp.pad(v, ((0, 0), (0, width - v.shape[1])))


def kernel(x, c, ada_w, ada_b, ln_g, ln_b, ffn_w_in, ffn_w_out, ev_w_in, ssd_conv_w, ssd_conv_b, ssd_dt_bias, ssd_a_log, ssd_d, ssd_norm_g, pool_w, pool_scale, ev_w_out, od_w_in, conf_dw_w, conf_dw_b, conf_ln_g, conf_ln_b, lru_conv_w, lru_conv_b, lru_wa, lru_ba, lru_wx, lru_bx, lru_lambda, od_w_out):
    t_len = x.shape[1]
    assert x.shape[2] == D_MODEL and t_len % min(TOKEN_TILE, t_len) == 0
    assert min(TOKEN_TILE, t_len) % SSD_CHUNK == 0

    mod = _ada_mod(c, ada_w, ada_b).reshape(DEPTH, x.shape[0], N_MOD, D_MODEL)

    ffn_w_in_b = ffn_w_in.astype(BF16)
    ffn_w_out_b = ffn_w_out.astype(BF16)
    o_dt = SSD_D_INNER + SSD_CONV_DIM
    ev_w_in_b = jnp.concatenate(
        [ev_w_in[:, :, :o_dt], ev_w_in[:, :, o_dt + SSD_HEADS:],
         jnp.pad(ev_w_in[:, :, o_dt:o_dt + SSD_HEADS], ((0, 0), (0, 0), (0, DT_PAD - SSD_HEADS)))],
        axis=-1).astype(BF16)
    ev_w_out_b = ev_w_out.astype(BF16)
    pool_w_b = pool_w.astype(BF16)
    od_w_in_b = od_w_in.astype(BF16)
    od_w_out_b = od_w_out.astype(BF16)
    wax_b = jnp.concatenate([lru_wa, lru_wx], axis=-1).astype(BF16)
    d_skip = jnp.repeat(ssd_d, SSD_HEAD_DIM, axis=1)

    for layer in range(DEPTH):
        x = _ffn(x, mod, ffn_w_in_b, ffn_w_out_b, ln_g, ln_b, layer, 0)
        if layer % 2 == 0:
            x = _even_mixer(x, mod, ev_w_in_b, ssd_conv_w, _row(ssd_conv_b),
                            _row(_pad_lanes(ssd_dt_bias, DT_PAD)), _row(_pad_lanes(ssd_a_log, DT_PAD)),
                            _row(d_skip), _row(ssd_norm_g), pool_w_b, _row(pool_scale), ev_w_out_b,
                            ln_g, ln_b, layer)
        else:
            x = _odd_mixer(x, mod, od_w_in_b, conf_dw_w, _row(conf_dw_b), _row(conf_ln_g), _row(conf_ln_b),
                           lru_conv_w, _row(lru_conv_b), wax_b, _row(lru_ba), _row(lru_bx), _row(lru_lambda),
                           od_w_out_b, ln_g, ln_b, layer)
        x = _ffn(x, mod, ffn_w_in_b, ffn_w_out_b, ln_g, ln_b, layer, 1)
    return x
```
